```python
import math, functools
import jax, jax.numpy as jnp
from jax import lax
import numpy as np

D_MODEL = 1024
BATCH = 16
SEQ = 2048
DEPTH = 4

D_MIX = D_MODEL
HEAD_DIM = 64
ATTN_WIDTH = D_MIX // 2
CONV_WIDTH = D_MIX // 4
POOL_WIDTH = D_MIX - ATTN_WIDTH - CONV_WIDTH
N_ATTN_HEADS = ATTN_WIDTH // HEAD_DIM
POOL_WINDOWS = (2, 4, 8, 16)
N_POOL_GROUPS = len(POOL_WINDOWS)
POOL_GROUP = POOL_WIDTH // N_POOL_GROUPS
CONV_K = 3
MOBA_BLOCK = 256
MOBA_TOP_K = 3
NUM_BUCKETS = 32
MAX_DISTANCE = 1024
D_FF = ((8 * D_MODEL // 3 + 255) // 256) * 256
IN_COLS = 3 * ATTN_WIDTH + 3 * CONV_WIDTH + POOL_WIDTH
DEEPNORM_ALPHA = (2.0 * DEPTH) ** 0.25
DEEPNORM_BETA = (8.0 * DEPTH) ** -0.25
LN_EPS = 1e-5
NEG_INF = -1e30

kernel_name = 'hymba_style_moba_conv_pool_deepnorm'


def layer_norm(x, g, b):
    xf = x.astype(jnp.float32)
    mu = xf.mean(-1, keepdims=True)
    var = jnp.square(xf - mu).mean(-1, keepdims=True)
    return ((xf - mu) * lax.rsqrt(var + LN_EPS)).astype(x.dtype) * g + b


def causal_dwconv(x, w):
    S = x.shape[1]
    xp = jnp.pad(x, ((0, 0), (CONV_K - 1, 0), (0, 0)))
    return sum(xp[:, j:j + S] * w[j] for j in range(CONV_K))


def t5_bucket(dist):
    n = jnp.maximum(dist, 0)
    max_exact = NUM_BUCKETS // 2
    nf = jnp.maximum(n, 1).astype(jnp.float32)
    large = max_exact + (jnp.log(nf / max_exact) / math.log(MAX_DISTANCE / max_exact)
                         * (NUM_BUCKETS - max_exact)).astype(jnp.int32)
    large = jnp.minimum(large, NUM_BUCKETS - 1)
    return jnp.where(n < max_exact, n, large)


def _moba_query_block(i, n_sel, args):
    qh, selh, kh, vh, tab = args
    L = MOBA_BLOCK
    scale = HEAD_DIM ** -0.5
    r = jnp.arange(L)
    rel = r[:, None] - r[None, :]
    own_logits = (jnp.einsum('qd,ld->ql', qh, kh[i]).astype(jnp.float32) * scale
                  + tab[t5_bucket(rel)].astype(jnp.float32))
    own_logits = jnp.where(rel >= 0, own_logits, NEG_INF)
    if n_sel == 0:
        p = jax.nn.softmax(own_logits, axis=-1).astype(vh.dtype)
        return jnp.einsum('ql,ld->qd', p, vh[i])
    k_sel = kh[selh]
    v_sel = vh[selh]
    dist = (i - selh)[:, :, None] * L + rel[:, None, :]
    sel_logits = (jnp.einsum('qd,qnld->qnl', qh, k_sel).astype(jnp.float32) * scale
                  + tab[t5_bucket(dist)].astype(jnp.float32))
    logits = jnp.concatenate([sel_logits.reshape(L, n_sel * L), own_logits], axis=-1)
    p = jax.nn.softmax(logits, axis=-1).astype(vh.dtype)
    p_sel = p[:, :n_sel * L].reshape(L, n_sel, L)
    p_own = p[:, n_sel * L:]
    return jnp.einsum('qnl,qnld->qd', p_sel, v_sel) + jnp.einsum('ql,ld->qd', p_own, vh[i])


def moba_attention(q, k, v, rel_bias):
    B, H, S, Dh = q.shape
    L = MOBA_BLOCK
    nb = -(-S // L)
    pad = nb * L - S
    if pad:
        cfg = ((0, 0), (0, 0), (0, pad), (0, 0))
        q, k, v = jnp.pad(q, cfg), jnp.pad(k, cfg), jnp.pad(v, cfg)
    qb = q.reshape(B, H, nb, L, Dh)
    kb = k.reshape(B, H, nb, L, Dh)
    vb = v.reshape(B, H, nb, L, Dh)
    k_mean = kb.astype(jnp.float32).mean(axis=3).astype(k.dtype)
    k_flat = kb.reshape(B * H, nb, L, Dh)
    v_flat = vb.reshape(B * H, nb, L, Dh)
    tab = jnp.broadcast_to(rel_bias[None], (B,) + rel_bias.shape).reshape(B * H, NUM_BUCKETS)
    outs = []
    for i in range(nb):
        q_i = qb[:, :, i]
        n_sel = min(MOBA_TOP_K, i)
        if n_sel:
            gate = jnp.einsum('bhqd,bhnd->bhqn', q_i, k_mean[:, :, :i]).astype(jnp.float32)
            _, sel = lax.top_k(gate, n_sel)
            sel = sel.reshape(B * H, L, n_sel)
        else:
            sel = jnp.zeros((B * H, L, 0), jnp.int32)
        step = functools.partial(_moba_query_block, i, n_sel)
        outs.append(lax.map(step, (q_i.reshape(B * H, L, Dh), sel, k_flat, v_flat, tab)))
    out = jnp.concatenate(outs, axis=1).reshape(B, H, nb * L, Dh)
    return out[:, :, :S]


def pool_mixer(p, w_pool, pool_scale):
    B, S, _ = p.shape
    pg = p.reshape(B, S, N_POOL_GROUPS, POOL_GROUP)
    pf = pg.astype(jnp.float32)
    c = jnp.cumsum(pf, axis=1)
    pos = jnp.arange(1, S + 1, dtype=jnp.float32)
    means = []
    for g, w in enumerate(POOL_WINDOWS):
        cg = c[:, :, g]
        lag = jnp.pad(cg, ((0, 0), (w, 0), (0, 0)))[:, :S]
        means.append((cg - lag) / jnp.minimum(pos, float(w))[None, :, None])
    pooled = (jnp.stack(means, axis=2) - pf).astype(p.dtype)
    y = jnp.einsum('bsgc,gcd->bsgd', pooled, w_pool)
    return y.reshape(B, S, POOL_WIDTH) * pool_scale


def to_heads(t):
    B, S, _ = t.shape
    return t.reshape(B, S, -1, HEAD_DIM).transpose(0, 2, 1, 3)


def hybrid_layer(x, w_in, conv_w, w_pool, pool_scale, w_out, ln1_g, ln1_b,
                 w_up, ffn_conv_w, ffn_conv_b, w_down, ln2_g, ln2_b, rel_bias):
    B, S, _ = x.shape
    h = x @ w_in
    sizes = [ATTN_WIDTH, ATTN_WIDTH, ATTN_WIDTH, CONV_WIDTH, CONV_WIDTH, CONV_WIDTH]
    q, k, v, cb, cc, cx, pin = jnp.split(h, list(np.cumsum(sizes)), axis=-1)
    attn = moba_attention(to_heads(q), to_heads(k), to_heads(v), rel_bias)
    attn = attn.transpose(0, 2, 1, 3).reshape(B, S, ATTN_WIDTH)
    conv = cb * causal_dwconv(cc * cx, conv_w)
    pool = pool_mixer(pin, w_pool, pool_scale)
    mix = jnp.concatenate([attn, conv, pool], axis=-1) @ w_out
    x = layer_norm(DEEPNORM_ALPHA * x + mix, ln1_g, ln1_b)
    up = causal_dwconv(x @ w_up, ffn_conv_w) + ffn_conv_b
    u, g = jnp.split(up, 2, axis=-1)
    ff = (u * jax.nn.silu(g)) @ w_down
    return layer_norm(DEEPNORM_ALPHA * x + ff, ln2_g, ln2_b)


def setup_inputs(seed: int = 0) -> dict:
    key = jax.random.key(seed)
    ks = jax.random.split(key, 16)
    nrm = jax.random.normal
    return {
        'x': nrm(ks[0], (BATCH, SEQ, D_MODEL), jnp.float32),
        'w_in': nrm(ks[1], (DEPTH, D_MODEL, IN_COLS), jnp.float32) * D_MODEL ** -0.5,
        'conv_w': nrm(ks[2], (DEPTH, CONV_K, CONV_WIDTH), jnp.float32) * CONV_K ** -0.5,
        'w_pool': nrm(ks[3], (DEPTH, N_POOL_GROUPS, POOL_GROUP, POOL_GROUP), jnp.float32) * POOL_GROUP ** -0.5,
        'pool_scale': 1.0 + 0.02 * nrm(ks[4], (DEPTH, POOL_WIDTH), jnp.float32),
        'w_out': nrm(ks[5], (DEPTH, D_MIX, D_MODEL), jnp.float32) * (D_MIX ** -0.5 * DEEPNORM_BETA),
        'ln1_g': 1.0 + 0.02 * nrm(ks[6], (DEPTH, D_MODEL), jnp.float32),
        'ln1_b': 0.02 * nrm(ks[7], (DEPTH, D_MODEL), jnp.float32),
        'w_up': nrm(ks[8], (DEPTH, D_MODEL, 2 * D_FF), jnp.float32) * D_MODEL ** -0.5,
        'ffn_conv_w': nrm(ks[9], (DEPTH, CONV_K, 2 * D_FF), jnp.float32) * CONV_K ** -0.5,
        'ffn_conv_b': 0.02 * nrm(ks[10], (DEPTH, 2 * D_FF), jnp.float32),
        'w_down': nrm(ks[11], (DEPTH, D_FF, D_MODEL), jnp.float32) * (D_FF ** -0.5 * DEEPNORM_BETA),
        'ln2_g': 1.0 + 0.02 * nrm(ks[12], (DEPTH, D_MODEL), jnp.float32),
        'ln2_b': 0.02 * nrm(ks[13], (DEPTH, D_MODEL), jnp.float32),
        'rel_bias': 0.5 * nrm(ks[14], (N_ATTN_HEADS, NUM_BUCKETS), jnp.float32),
    }


def reference(x, w_in, conv_w, w_pool, pool_scale, w_out, ln1_g, ln1_b,
              w_up, ffn_conv_w, ffn_conv_b, w_down, ln2_g, ln2_b, rel_bias):
    for l in range(DEPTH):
        x = hybrid_layer(x, w_in[l], conv_w[l], w_pool[l], pool_scale[l], w_out[l],
                         ln1_g[l], ln1_b[l], w_up[l], ffn_conv_w[l], ffn_conv_b[l],
                         w_down[l], ln2_g[l], ln2_b[l], rel_bias)
    return x
```

```python
import functools
import math

import jax
import jax.numpy as jnp
from jax import lax
from jax.experimental import pallas as pl
from jax.experimental.pallas import tpu as pltpu

HEAD_DIM = 64
N_ATTN_HEADS = 8
ATTN_WIDTH = N_ATTN_HEADS * HEAD_DIM
CONV_WIDTH = 256
POOL_WIDTH = 256
POOL_WINDOWS = (2, 4, 8, 16)
POOL_GROUP = POOL_WIDTH // len(POOL_WINDOWS)
CONV_K = 3
MOBA_BLOCK = 256
MOBA_TOP_K = 3
NUM_BUCKETS = 32
MAX_DISTANCE = 1024
LN_EPS = 1e-5
NEG_INF = -1e30

LANES = 128
HEADS_PER_LANE_TILE = LANES // HEAD_DIM
HALO_ROWS = 16
CARRY_ROWS = 8
VMEM_LIMIT_BYTES = 56 * 1024 * 1024

_BF16 = jnp.bfloat16
_F32 = jnp.float32


def _t5_bucket(dist):
    n = jnp.maximum(dist, 0)
    max_exact = NUM_BUCKETS // 2
    nf = jnp.maximum(n, 1).astype(jnp.float32)
    large = max_exact + (jnp.log(nf / max_exact) / math.log(MAX_DISTANCE / max_exact)
                         * (NUM_BUCKETS - max_exact)).astype(jnp.int32)
    large = jnp.minimum(large, NUM_BUCKETS - 1)
    return jnp.where(n < max_exact, n, large)


def _layer_norm(z, g, b):
    mu = jnp.mean(z, axis=-1, keepdims=True)
    zc = z - mu
    var = jnp.mean(zc * zc, axis=-1, keepdims=True)
    return zc * lax.rsqrt(var + LN_EPS) * g + b


def _bias_kernel(tab_ref, bucket_ref, out_ref):
    h = pl.program_id(0)
    d = pl.program_id(1)
    L = MOBA_BLOCK
    bucket = bucket_ref[0]
    acc = jnp.zeros((L, L), _F32)
    for b in range(NUM_BUCKETS):
        acc = jnp.where(bucket == b, tab_ref[h, b], acc)
    key = lax.broadcasted_iota(jnp.int32, (L, L), 0)
    qry = lax.broadcasted_iota(jnp.int32, (L, L), 1)
    out_ref[0, 0] = jnp.where(d * L + qry - key >= 0, acc, NEG_INF)


def _build_bias(rel_bias, nb):
    L = MOBA_BLOCK
    key = jnp.arange(L, dtype=jnp.int32)[:, None]
    qry = jnp.arange(L, dtype=jnp.int32)[None, :]
    delta = jnp.arange(nb, dtype=jnp.int32)[:, None, None]
    bucket_t = _t5_bucket(delta * L + qry - key)
    return pl.pallas_call(
        _bias_kernel,
        grid=(N_ATTN_HEADS, nb),
        in_specs=[pl.BlockSpec(memory_space=pltpu.SMEM),
                  pl.BlockSpec((1, L, L), lambda h, d: (d, 0, 0))],
        out_specs=pl.BlockSpec((1, 1, L, L), lambda h, d: (h, d, 0, 0)),
        out_shape=jax.ShapeDtypeStruct((N_ATTN_HEADS, nb, L, L), _F32),
        name="rel_bias",
    )(rel_bias, bucket_t)


def _inproj_kernel(x_ref, w_ref, qkv_ref, cp_ref, kmean_ref, *, tm):
    L = MOBA_BLOCK
    xb = x_ref[0].astype(_BF16)
    scale = HEAD_DIM ** -0.5

    def proj(c0, width):
        return jnp.dot(xb, w_ref[:, c0:c0 + width], preferred_element_type=_F32)

    qkv_ref[0, :, 0:ATTN_WIDTH] = (proj(0, ATTN_WIDTH) * scale).astype(_BF16)
    k = proj(ATTN_WIDTH, ATTN_WIDTH)
    qkv_ref[0, :, ATTN_WIDTH:2 * ATTN_WIDTH] = k.astype(_BF16)
    for j in range(tm // L):
        kmean_ref[0, j] = jnp.mean(k[j * L:(j + 1) * L], axis=0, keepdims=True)
    qkv_ref[0, :, 2 * ATTN_WIDTH:3 * ATTN_WIDTH] = proj(2 * ATTN_WIDTH, ATTN_WIDTH).astype(_BF16)
    rest = 3 * CONV_WIDTH + POOL_WIDTH
    for c in range(0, rest, 512):
        cp_ref[0, :, c:c + 512] = proj(3 * ATTN_WIDTH + c, 512)


def _in_proj(x, w_in_bf16, tm):
    B, S, D = x.shape
    L = MOBA_BLOCK
    cols = w_in_bf16.shape[1]
    rest = cols - 3 * ATTN_WIDTH
    return pl.pallas_call(
        functools.partial(_inproj_kernel, tm=tm),
        grid=(B, S // tm),
        in_specs=[pl.BlockSpec((1, tm, D), lambda b, s: (b, s, 0)),
                  pl.BlockSpec((D, cols), lambda b, s: (0, 0))],
        out_specs=[pl.BlockSpec((1, tm, 3 * ATTN_WIDTH), lambda b, s: (b, s, 0)),
                   pl.BlockSpec((1, tm, rest), lambda b, s: (b, s, 0)),
                   pl.BlockSpec((1, tm // L, 1, ATTN_WIDTH), lambda b, s: (b, s, 0, 0))],
        out_shape=[jax.ShapeDtypeStruct((B, S, 3 * ATTN_WIDTH), _BF16),
                   jax.ShapeDtypeStruct((B, S, rest), _F32),
                   jax.ShapeDtypeStruct((B, S // L, 1, ATTN_WIDTH), _F32)],
        compiler_params=pltpu.CompilerParams(
            dimension_semantics=("parallel", "parallel"), vmem_limit_bytes=VMEM_LIMIT_BYTES),
        name="in_proj",
    )(x, w_in_bf16)


def _attn_kernel(q_ref, k_ref, v_ref, km_ref, bias_ref, o_ref, *, nb):
    L = MOBA_BLOCK
    hh = pl.program_id(2)
    lane = lax.broadcasted_iota(jnp.int32, (1, LANES), 1)
    in_head = (lane >= hh * HEAD_DIM) & (lane < (hh + 1) * HEAD_DIM)
    k_h = jnp.where(in_head, k_ref[0], jnp.zeros((), _BF16))
    km_h = jnp.where(in_head, km_ref[0], 0.0).astype(_BF16)
    v_t = jnp.transpose(v_ref[0].astype(_F32)).astype(_BF16)
    blk = lax.broadcasted_iota(jnp.int32, (nb, 1), 0)
    nt = (((1,), (1,)), ((), ()))

    for i in range(nb):
        n = (i + 1) * L
        q_i = q_ref[0, i * L:(i + 1) * L, :]
        s_t = lax.dot_general(k_h[:n], q_i, nt, preferred_element_type=_F32)
        n_sel = min(MOBA_TOP_K, i)
        if n_sel < i:
            gate = lax.dot_general(km_h, q_i, nt, preferred_element_type=_F32)
            rank = jnp.zeros((nb, L), _F32)
            for jp in range(i):
                gj = gate[jp:jp + 1, :]
                beats = (gj > gate) | ((gj == gate) & (blk > jp))
                rank = rank + jnp.where(beats, 1.0, 0.0)
            sel = jnp.where(rank < n_sel, 1.0, 0.0)
        z = []
        for j in range(i + 1):
            zj = s_t[j * L:(j + 1) * L] + bias_ref[0, i - j]
            if j < i and n_sel < i:
                zj = jnp.where(sel[j:j + 1, :] > 0.5, zj, NEG_INF)
            z.append(zj)
        m = z[0].max(axis=0, keepdims=True)
        for zj in z[1:]:
            m = jnp.maximum(m, zj.max(axis=0, keepdims=True))
        p = [jnp.exp(zj - m) for zj in z]
        l = p[0].sum(axis=0, keepdims=True)
        for pj in p[1:]:
            l = l + pj.sum(axis=0, keepdims=True)
        p_t = jnp.concatenate([pj.astype(_BF16) for pj in p], axis=0)
        o_t = jnp.dot(v_t[:, :n], p_t, preferred_element_type=_F32) / l
        o_i = jnp.transpose(o_t)
        rows = slice(i * L, (i + 1) * L)

        @pl.when(hh == 0)
        def _():
            o_ref[0, rows, :] = o_i.astype(o_ref.dtype)

        @pl.when(hh != 0)
        def _():
            o_ref[0, rows, :] = jnp.where(in_head, o_i.astype(o_ref.dtype), o_ref[0, rows, :])


def _attention(qkv, kmean, bias_t):
    B, S, _ = qkv.shape
    L = MOBA_BLOCK
    nb = S // L
    n_tiles = ATTN_WIDTH // LANES
    return pl.pallas_call(
        functools.partial(_attn_kernel, nb=nb),
        grid=(n_tiles, B, HEADS_PER_LANE_TILE),
        in_specs=[pl.BlockSpec((1, S, LANES), lambda t, b, h: (b, 0, t)),
                  pl.BlockSpec((1, S, LANES), lambda t, b, h: (b, 0, n_tiles + t)),
                  pl.BlockSpec((1, S, LANES), lambda t, b, h: (b, 0, 2 * n_tiles + t)),
                  pl.BlockSpec((1, nb, LANES), lambda t, b, h: (b, 0, t)),
                  pl.BlockSpec((1, nb, L, L), lambda t, b, h: (t * HEADS_PER_LANE_TILE + h, 0, 0, 0))],
        out_specs=pl.BlockSpec((1, S, LANES), lambda t, b, h: (b, 0, t)),
        out_shape=jax.ShapeDtypeStruct((B, S, ATTN_WIDTH), _BF16),
        compiler_params=pltpu.CompilerParams(
            dimension_semantics=("parallel", "parallel", "arbitrary"), vmem_limit_bytes=VMEM_LIMIT_BYTES),
        name="moba_attention",
    )(qkv, qkv, qkv, kmean, bias_t)


def _mix_kernel(attn_ref, cp_ref, halo_ref, x_ref, wout_ref, convw_ref, wpool_ref, pscale_ref,
                g_ref, b_ref, o_ref, *, tm, alpha):
    s = pl.program_id(1)
    H = HALO_ROWS
    cp = cp_ref[0]
    halo = jnp.where(s > 0, halo_ref[0], 0.0)
    c1, c2, c3 = CONV_WIDTH, 2 * CONV_WIDTH, 3 * CONV_WIDTH

    prod = cp[:, c1:c2] * cp[:, c2:c3]
    ext = jnp.concatenate([halo[:, c1:c2] * halo[:, c2:c3], prod], axis=0)
    cw = convw_ref[...]
    conv = cp[:, 0:c1] * (cw[0:1] * pltpu.roll(ext, 2, 0)[H:] + cw[1:2] * pltpu.roll(ext, 1, 0)[H:]
                          + cw[2:3] * prod)

    pin = cp[:, c3:]
    acc = jnp.concatenate([halo[:, c3:], pin], axis=0)
    col = lax.broadcasted_iota(jnp.int32, (1, POOL_WIDTH), 1)
    wsum = None
    wlen = None
    shift = 1
    for g, w in enumerate(POOL_WINDOWS):
        while shift < w:
            acc = acc + pltpu.roll(acc, shift, 0)
            shift *= 2
        if wsum is None:
            wsum, wlen = acc, jnp.full((1, POOL_WIDTH), float(w), _F32)
        else:
            in_group = col >= g * POOL_GROUP
            wsum = jnp.where(in_group, acc, wsum)
            wlen = jnp.where(in_group, float(w), wlen)
    pos = (s * tm + 1 + lax.broadcasted_iota(jnp.int32, (tm, 1), 0)).astype(_F32)
    pooled = wsum[H:] / jnp.minimum(pos, wlen) - pin
    pool = jnp.dot(pooled.astype(_BF16), wpool_ref[...], preferred_element_type=_F32) * pscale_ref[...]

    mix_in = jnp.concatenate([attn_ref[0], conv.astype(_BF16), pool.astype(_BF16)], axis=1)
    mix = jnp.dot(mix_in, wout_ref[...], preferred_element_type=_F32)
    o_ref[0] = _layer_norm(alpha * x_ref[0] + mix, g_ref[...], b_ref[...])


def _mix(attn, cp, x, w_out_bf16, conv_w, w_pool_bd, pool_scale, ln_g, ln_b, tm, alpha):
    B, S, D = x.shape
    H = HALO_ROWS
    cpw = cp.shape[2]
    const = lambda b, s: (0, 0)
    return pl.pallas_call(
        functools.partial(_mix_kernel, tm=tm, alpha=alpha),
        grid=(B, S // tm),
        in_specs=[pl.BlockSpec((1, tm, ATTN_WIDTH), lambda b, s: (b, s, 0)),
                  pl.BlockSpec((1, tm, cpw), lambda b, s: (b, s, 0)),
                  pl.BlockSpec((1, H, cpw), lambda b, s: (b, jnp.maximum(s * (tm // H) - 1, 0), 0)),
                  pl.BlockSpec((1, tm, D), lambda b, s: (b, s, 0)),
                  pl.BlockSpec(w_out_bf16.shape, const),
                  pl.BlockSpec(conv_w.shape, const),
                  pl.BlockSpec(w_pool_bd.shape, const),
                  pl.BlockSpec(pool_scale.shape, const),
                  pl.BlockSpec(ln_g.shape, const),
                  pl.BlockSpec(ln_b.shape, const)],
        out_specs=pl.BlockSpec((1, tm, D), lambda b, s: (b, s, 0)),
        out_shape=jax.ShapeDtypeStruct((B, S, D), _F32),
        compiler_params=pltpu.CompilerParams(
            dimension_semantics=("parallel", "parallel"), vmem_limit_bytes=VMEM_LIMIT_BYTES),
        name="mix_proj",
    )(attn, cp, cp, x, w_out_bf16, conv_w, w_pool_bd, pool_scale, ln_g, ln_b)


def _ffn_kernel(x_ref, wup_ref, cw_ref, cb_ref, wdn_ref, g_ref, b_ref, o_ref, act_ref, carry_ref,
                *, tm, tf, dff, alpha):
    s = pl.program_id(1)
    R = CARRY_ROWS
    x = x_ref[0]
    xb = x.astype(_BF16)

    def conv_up(c0):
        up = jnp.dot(xb, wup_ref[:, c0:c0 + tf], preferred_element_type=_F32)
        prev = jnp.where(s > 0, carry_ref[:, c0:c0 + tf], 0.0)
        carry_ref[:, c0:c0 + tf] = up[tm - R:]
        ext = jnp.concatenate([prev, up], axis=0)
        w = cw_ref[:, c0:c0 + tf]
        return (w[0:1] * pltpu.roll(ext, 2, 0)[R:] + w[1:2] * pltpu.roll(ext, 1, 0)[R:]
                + w[2:3] * up + cb_ref[:, c0:c0 + tf])

    for f in range(dff // tf):
        u = conv_up(f * tf)
        g = conv_up(dff + f * tf)
        act_ref[:, f * tf:(f + 1) * tf] = (u * (g * jax.nn.sigmoid(g))).astype(_BF16)
    ff = jnp.dot(act_ref[...], wdn_ref[...], preferred_element_type=_F32)
    o_ref[0] = _layer_norm(alpha * x + ff, g_ref[...], b_ref[...])


def _ffn(x, w_up_bf16, ffn_conv_w, ffn_conv_b, w_down_bf16, ln_g, ln_b, tm, tf, alpha):
    B, S, D = x.shape
    dff = w_down_bf16.shape[0]
    const = lambda b, s: (0, 0)
    return pl.pallas_call(
        functools.partial(_ffn_kernel, tm=tm, tf=tf, dff=dff, alpha=alpha),
        grid=(B, S // tm),
        in_specs=[pl.BlockSpec((1, tm, D), lambda b, s: (b, s, 0)),
                  pl.BlockSpec(w_up_bf16.shape, const, pipeline_mode=pl.Buffered(1)),
                  pl.BlockSpec(ffn_conv_w.shape, const),
                  pl.BlockSpec(ffn_conv_b.shape, const),
                  pl.BlockSpec(w_down_bf16.shape, const, pipeline_mode=pl.Buffered(1)),
                  pl.BlockSpec(ln_g.shape, const),
                  pl.BlockSpec(ln_b.shape, const)],
        out_specs=pl.BlockSpec((1, tm, D), lambda b, s: (b, s, 0)),
        out_shape=jax.ShapeDtypeStruct((B, S, D), _F32),
        scratch_shapes=[pltpu.VMEM((tm, dff), _BF16),
                        pltpu.VMEM((CARRY_ROWS, 2 * dff), _F32)],
        compiler_params=pltpu.CompilerParams(
            dimension_semantics=("parallel", "arbitrary"), vmem_limit_bytes=VMEM_LIMIT_BYTES),
        name="conv_ffn",
    )(x, w_up_bf16, ffn_conv_w, ffn_conv_b, w_down_bf16, ln_g, ln_b)


def _block_diag(w_pool):
    g, c, d = w_pool.shape
    out = jnp.zeros((g * c, g * d), w_pool.dtype)
    for i in range(g):
        out = out.at[i * c:(i + 1) * c, i * d:(i + 1) * d].set(w_pool[i])
    return out


def kernel(x, w_in, conv_w, w_pool, pool_scale, w_out, ln1_g, ln1_b, w_up, ffn_conv_w, ffn_conv_b,
           w_down, ln2_g, ln2_b, rel_bias):
    depth = w_in.shape[0]
    B, S, D = x.shape
    L = MOBA_BLOCK
    assert S % L == 0 and w_in.shape[2] == 3 * ATTN_WIDTH + 3 * CONV_WIDTH + POOL_WIDTH
    alpha = (2.0 * depth) ** 0.25
    tm = 512
    tf = 256
    bias_t = _build_bias(rel_bias, S // L)
    for l in range(depth):
        qkv, cp, kmean = _in_proj(x, w_in[l].astype(_BF16), tm)
        attn = _attention(qkv, kmean.reshape(B, S // L, ATTN_WIDTH), bias_t)
        x = _mix(attn, cp, x, w_out[l].astype(_BF16), conv_w[l], _block_diag(w_pool[l]).astype(_BF16),
                 pool_scale[l][None, :], ln1_g[l][None, :], ln1_b[l][None, :], tm, alpha)
        x = _ffn(x, w_up[l].astype(_BF16), ffn_conv_w[l], ffn_conv_b[l][None, :], w_down[l].astype(_BF16),
                 ln2_g[l][None, :], ln2_b[l][None, :], tm, tf, alpha)
    return x
```

```python
import functools
import math

import jax
import jax.numpy as jnp
from jax import lax
from jax.experimental import pallas as pl
from jax.experimental.pallas import tpu as pltpu

HEAD_DIM = 64
N_ATTN_HEADS = 8
ATTN_WIDTH = N_ATTN_HEADS * HEAD_DIM
CONV_WIDTH = 256
POOL_WIDTH = 256
POOL_WINDOWS = (2, 4, 8, 16)
POOL_GROUP = POOL_WIDTH // len(POOL_WINDOWS)
CONV_K = 3
MOBA_BLOCK = 256
MOBA_TOP_K = 3
NUM_BUCKETS = 32
MAX_DISTANCE = 1024
LN_EPS = 1e-5
NEG_INF = -1e30

LANES = 128
HEADS_PER_LANE_TILE = LANES // HEAD_DIM
HALO_ROWS = 16
CARRY_ROWS = 8
GATE_ROWS = 16
VMEM_LIMIT_BYTES = 56 * 1024 * 1024

_BF16 = jnp.bfloat16
_F32 = jnp.float32


def _t5_bucket(dist):
    n = jnp.maximum(dist, 0)
    max_exact = NUM_BUCKETS // 2
    nf = jnp.maximum(n, 1).astype(jnp.float32)
    large = max_exact + (jnp.log(nf / max_exact) / math.log(MAX_DISTANCE / max_exact)
                         * (NUM_BUCKETS - max_exact)).astype(jnp.int32)
    large = jnp.minimum(large, NUM_BUCKETS - 1)
    return jnp.where(n < max_exact, n, large)


def _layer_norm(z, g, b):
    mu = jnp.mean(z, axis=-1, keepdims=True)
    zc = z - mu
    var = jnp.mean(zc * zc, axis=-1, keepdims=True)
    return zc * lax.rsqrt(var + LN_EPS) * g + b


def _bias_kernel(tab_ref, bucket_ref, out_ref):
    h = pl.program_id(0)
    d = pl.program_id(1)
    L = MOBA_BLOCK
    bucket = bucket_ref[0]
    acc = jnp.zeros((L, L), _F32)
    for b in range(NUM_BUCKETS):
        acc = jnp.where(bucket == b, tab_ref[h, b], acc)
    key = lax.broadcasted_iota(jnp.int32, (L, L), 0)
    qry = lax.broadcasted_iota(jnp.int32, (L, L), 1)
    out_ref[0, 0] = jnp.where(d * L + qry - key >= 0, acc, NEG_INF)


def _build_bias(rel_bias, nb):
    L = MOBA_BLOCK
    key = jnp.arange(L, dtype=jnp.int32)[:, None]
    qry = jnp.arange(L, dtype=jnp.int32)[None, :]
    delta = jnp.arange(nb, dtype=jnp.int32)[:, None, None]
    bucket_t = _t5_bucket(delta * L + qry - key)
    return pl.pallas_call(
        _bias_kernel,
        grid=(N_ATTN_HEADS, nb),
        in_specs=[pl.BlockSpec(memory_space=pltpu.SMEM),
                  pl.BlockSpec((1, L, L), lambda h, d: (d, 0, 0))],
        out_specs=pl.BlockSpec((1, 1, L, L), lambda h, d: (h, d, 0, 0)),
        out_shape=jax.ShapeDtypeStruct((N_ATTN_HEADS, nb, L, L), _F32),
        name="rel_bias",
    )(rel_bias, bucket_t)


def _inproj_kernel(x_ref, w_ref, qkv_ref, cp_ref, kmean_ref, *, tm):
    L = MOBA_BLOCK
    xb = x_ref[0].astype(_BF16)
    scale = HEAD_DIM ** -0.5

    def proj(c0, width):
        return jnp.dot(xb, w_ref[:, c0:c0 + width], preferred_element_type=_F32)

    qkv_ref[0, :, 0:ATTN_WIDTH] = (proj(0, ATTN_WIDTH) * scale).astype(_BF16)
    k = proj(ATTN_WIDTH, ATTN_WIDTH)
    qkv_ref[0, :, ATTN_WIDTH:2 * ATTN_WIDTH] = k.astype(_BF16)
    for j in range(tm // L):
        kmean_ref[0, j] = jnp.mean(k[j * L:(j + 1) * L], axis=0, keepdims=True)
    qkv_ref[0, :, 2 * ATTN_WIDTH:3 * ATTN_WIDTH] = proj(2 * ATTN_WIDTH, ATTN_WIDTH).astype(_BF16)
    rest = 3 * CONV_WIDTH + POOL_WIDTH
    for c in range(0, rest, 512):
        cp_ref[0, :, c:c + 512] = proj(3 * ATTN_WIDTH + c, 512)


def _in_proj(x, w_in_bf16, tm):
    B, S, D = x.shape
    L = MOBA_BLOCK
    cols = w_in_bf16.shape[1]
    rest = cols - 3 * ATTN_WIDTH
    return pl.pallas_call(
        functools.partial(_inproj_kernel, tm=tm),
        grid=(B, S // tm),
        in_specs=[pl.BlockSpec((1, tm, D), lambda b, s: (b, s, 0)),
                  pl.BlockSpec((D, cols), lambda b, s: (0, 0))],
        out_specs=[pl.BlockSpec((1, tm, 3 * ATTN_WIDTH), lambda b, s: (b, s, 0)),
                   pl.BlockSpec((1, tm, rest), lambda b, s: (b, s, 0)),
                   pl.BlockSpec((1, tm // L, 1, ATTN_WIDTH), lambda b, s: (b, s, 0, 0))],
        out_shape=[jax.ShapeDtypeStruct((B, S, 3 * ATTN_WIDTH), _BF16),
                   jax.ShapeDtypeStruct((B, S, rest), _F32),
                   jax.ShapeDtypeStruct((B, S // L, 1, ATTN_WIDTH), _F32)],
        compiler_params=pltpu.CompilerParams(
            dimension_semantics=("parallel", "parallel"), vmem_limit_bytes=VMEM_LIMIT_BYTES),
        name="in_proj",
    )(x, w_in_bf16)


def _attn_kernel(q_ref, k_ref, v_ref, km_ref, bias_ref, o_ref, keys_ref, vals_ref, z_ref, p_ref, *, nb):
    L = MOBA_BLOCK
    G = GATE_ROWS
    lane = lax.broadcasted_iota(jnp.int32, (1, LANES), 1)
    dim = lax.broadcasted_iota(jnp.int32, (LANES, 1), 0)
    blk = lax.broadcasted_iota(jnp.int32, (nb, 1), 0)
    nt = (((1,), (1,)), ((), ()))
    NH = HEADS_PER_LANE_TILE
    k_all = k_ref[0]
    km = km_ref[0]
    v_t = jnp.transpose(v_ref[0].astype(_F32))
    for hh in range(NH):
        in_head = (lane >= hh * HEAD_DIM) & (lane < (hh + 1) * HEAD_DIM)
        keys_ref[hh, 0:G, :] = jnp.concatenate(
            [jnp.where(in_head, km, 0.0), jnp.zeros((G - nb, LANES), _F32)], axis=0).astype(_BF16)
        keys_ref[hh, G:, :] = jnp.where(in_head, k_all, jnp.zeros((), _BF16))
        in_rows = (dim >= hh * HEAD_DIM) & (dim < (hh + 1) * HEAD_DIM)
        vals_ref[hh] = jnp.where(in_rows, v_t, 1.0).astype(_BF16)

    def score_steps(c, i, hh, state):
        slot = c % 2
        n_sel = min(MOBA_TOP_K, i)
        q_i = q_ref[0, i * L:(i + 1) * L, :]

        def step(j):
            lo = 0 if j == 0 else G + j * L
            s = lax.dot_general(keys_ref[hh, lo:G + (j + 1) * L, :], q_i, nt, preferred_element_type=_F32)
            if j == 0:
                if n_sel < i:
                    gate = s[:nb]
                    rank = jnp.zeros((nb, L), _F32)
                    for jp in range(i):
                        gj = gate[jp:jp + 1, :]
                        beats = (gj > gate) | ((gj == gate) & (blk > jp))
                        rank = rank + jnp.where(beats, 1.0, 0.0)
                    state["sel"] = jnp.where(rank < n_sel, 1.0, 0.0)
                s = s[G:]
            zj = s + bias_ref[hh, i - j]
            if j < i and n_sel < i:
                zj = jnp.where(state["sel"][j:j + 1, :] > 0.5, zj, NEG_INF)
            z_ref[slot, j * L:(j + 1) * L, :] = zj
            m8 = jnp.max(zj.reshape(L // 8, 8, L), axis=0)
            state["m8"] = m8 if j == 0 else jnp.maximum(state["m8"], m8)
            if j == i:
                state["m"] = jnp.max(state["m8"], axis=0, keepdims=True)

        return [functools.partial(step, j) for j in range(i + 1)]

    def prob_steps(c, i, hh, state, outs):
        slot = c % 2
        n = (i + 1) * L

        def step(j):
            rows = slice(j * L, (j + 1) * L)
            p_ref[slot, rows, :] = jnp.exp(z_ref[slot, rows, :] - state["m"]).astype(_BF16)
            if j == i:
                o_ext = jnp.dot(vals_ref[hh, :, 0:n], p_ref[slot, 0:n, :], preferred_element_type=_F32)
                norm_row = (1 - hh) * HEAD_DIM
                outs[hh] = o_ext * (1.0 / o_ext[norm_row:norm_row + 1, :])
                if hh == NH - 1:
                    o_t = jnp.where(dim < HEAD_DIM, outs[0], outs[1])
                    o_ref[0, i * L:(i + 1) * L, :] = jnp.transpose(o_t).astype(o_ref.dtype)

        return [functools.partial(step, j) for j in range(i + 1)]

    chains = [(i, hh) for i in range(nb) for hh in range(NH)]
    pending = []
    outs = {}
    for c, (i, hh) in enumerate(chains):
        state = {}
        if hh == 0:
            outs = {}
        score = score_steps(c, i, hh, state)
        for t in range(max(len(score), len(pending))):
            if t < len(score):
                score[t]()
            if t < len(pending):
                pending[t]()
        pending = prob_steps(c, i, hh, state, outs)
    for step in pending:
        step()


def _attention(qkv, kmean, bias_t):
    B, S, _ = qkv.shape
    L = MOBA_BLOCK
    nb = S // L
    assert nb <= GATE_ROWS and HEADS_PER_LANE_TILE == 2
    n_tiles = ATTN_WIDTH // LANES
    return pl.pallas_call(
        functools.partial(_attn_kernel, nb=nb),
        grid=(n_tiles, B),
        in_specs=[pl.BlockSpec((1, S, LANES), lambda t, b: (b, 0, t)),
                  pl.BlockSpec((1, S, LANES), lambda t, b: (b, 0, n_tiles + t)),
                  pl.BlockSpec((1, S, LANES), lambda t, b: (b, 0, 2 * n_tiles + t)),
                  pl.BlockSpec((1, nb, LANES), lambda t, b: (b, 0, t)),
                  pl.BlockSpec((HEADS_PER_LANE_TILE, nb, L, L), lambda t, b: (t, 0, 0, 0))],
        out_specs=pl.BlockSpec((1, S, LANES), lambda t, b: (b, 0, t)),
        out_shape=jax.ShapeDtypeStruct((B, S, ATTN_WIDTH), _BF16),
        scratch_shapes=[pltpu.VMEM((HEADS_PER_LANE_TILE, GATE_ROWS + S, LANES), _BF16),
                        pltpu.VMEM((HEADS_PER_LANE_TILE, LANES, S), _BF16),
                        pltpu.VMEM((2, S, L), _F32),
                        pltpu.VMEM((2, S, L), _BF16)],
        compiler_params=pltpu.CompilerParams(
            dimension_semantics=("parallel", "parallel"), vmem_limit_bytes=VMEM_LIMIT_BYTES),
        name="moba_attention",
    )(qkv, qkv, qkv, kmean, bias_t)


def _mix_kernel(attn_ref, cp_ref, halo_ref, x_ref, wout_ref, convw_ref, wpool_ref, pscale_ref,
                g_ref, b_ref, o_ref, *, tm, alpha):
    s = pl.program_id(1)
    H = HALO_ROWS
    cp = cp_ref[0]
    halo = jnp.where(s > 0, halo_ref[0], 0.0)
    c1, c2, c3 = CONV_WIDTH, 2 * CONV_WIDTH, 3 * CONV_WIDTH

    prod = cp[:, c1:c2] * cp[:, c2:c3]
    ext = jnp.concatenate([halo[:, c1:c2] * halo[:, c2:c3], prod], axis=0)
    cw = convw_ref[...]
    conv = cp[:, 0:c1] * (cw[0:1] * pltpu.roll(ext, 2, 0)[H:] + cw[1:2] * pltpu.roll(ext, 1, 0)[H:]
                          + cw[2:3] * prod)

    pin = cp[:, c3:]
    acc = jnp.concatenate([halo[:, c3:], pin], axis=0)
    col = lax.broadcasted_iota(jnp.int32, (1, POOL_WIDTH), 1)
    wsum = None
    wlen = None
    shift = 1
    for g, w in enumerate(POOL_WINDOWS):
        while shift < w:
            acc = acc + pltpu.roll(acc, shift, 0)
            shift *= 2
        if wsum is None:
            wsum, wlen = acc, jnp.full((1, POOL_WIDTH), float(w), _F32)
        else:
            in_group = col >= g * POOL_GROUP
            wsum = jnp.where(in_group, acc, wsum)
            wlen = jnp.where(in_group, float(w), wlen)
    pos = (s * tm + 1 + lax.broadcasted_iota(jnp.int32, (tm, 1), 0)).astype(_F32)
    pooled = wsum[H:] / jnp.minimum(pos, wlen) - pin
    pool = jnp.dot(pooled.astype(_BF16), wpool_ref[...], preferred_element_type=_F32) * pscale_ref[...]

    mix_in = jnp.concatenate([attn_ref[0], conv.astype(_BF16), pool.astype(_BF16)], axis=1)
    mix = jnp.dot(mix_in, wout_ref[...], preferred_element_type=_F32)
    o_ref[0] = _layer_norm(alpha * x_ref[0] + mix, g_ref[...], b_ref[...])


def _mix(attn, cp, x, w_out_bf16, conv_w, w_pool_bd, pool_scale, ln_g, ln_b, tm, alpha):
    B, S, D = x.shape
    H = HALO_ROWS
    cpw = cp.shape[2]
    const = lambda b, s: (0, 0)
    return pl.pallas_call(
        functools.partial(_mix_kernel, tm=tm, alpha=alpha),
        grid=(B, S // tm),
        in_specs=[pl.BlockSpec((1, tm, ATTN_WIDTH), lambda b, s: (b, s, 0)),
                  pl.BlockSpec((1, tm, cpw), lambda b, s: (b, s, 0)),
                  pl.BlockSpec((1, H, cpw), lambda b, s: (b, jnp.maximum(s * (tm // H) - 1, 0), 0)),
                  pl.BlockSpec((1, tm, D), lambda b, s: (b, s, 0)),
                  pl.BlockSpec(w_out_bf16.shape, const),
                  pl.BlockSpec(conv_w.shape, const),
                  pl.BlockSpec(w_pool_bd.shape, const),
                  pl.BlockSpec(pool_scale.shape, const),
                  pl.BlockSpec(ln_g.shape, const),
                  pl.BlockSpec(ln_b.shape, const)],
        out_specs=pl.BlockSpec((1, tm, D), lambda b, s: (b, s, 0)),
        out_shape=jax.ShapeDtypeStruct((B, S, D), _F32),
        compiler_params=pltpu.CompilerParams(
            dimension_semantics=("parallel", "parallel"), vmem_limit_bytes=VMEM_LIMIT_BYTES),
        name="mix_proj",
    )(attn, cp, cp, x, w_out_bf16, conv_w, w_pool_bd, pool_scale, ln_g, ln_b)


def _ffn_kernel(x_ref, wup_ref, cw_ref, cb_ref, wdn_ref, g_ref, b_ref, o_ref, act_ref, carry_ref,
                *, tm, tf, dff, alpha):
    s = pl.program_id(1)
    R = CARRY_ROWS
    x = x_ref[0]
    xb = x.astype(_BF16)

    def conv_up(c0):
        up = jnp.dot(xb, wup_ref[:, c0:c0 + tf], preferred_element_type=_F32)
        prev = jnp.where(s > 0, carry_ref[:, c0:c0 + tf], 0.0)
        carry_ref[:, c0:c0 + tf] = up[tm - R:]
        ext = jnp.concatenate([prev, up], axis=0)
        w = cw_ref[:, c0:c0 + tf]
        return (w[0:1] * pltpu.roll(ext, 2, 0)[R:] + w[1:2] * pltpu.roll(ext, 1, 0)[R:]
                + w[2:3] * up + cb_ref[:, c0:c0 + tf])

    for f in range(dff // tf):
        u = conv_up(f * tf)
        g = conv_up(dff + f * tf)
        act_ref[:, f * tf:(f + 1) * tf] = (u * (g * jax.nn.sigmoid(g))).astype(_BF16)
    ff = jnp.dot(act_ref[...], wdn_ref[...], preferred_element_type=_F32)
    o_ref[0] = _layer_norm(alpha * x + ff, g_ref[...], b_ref[...])


def _ffn(x, w_up_bf16, ffn_conv_w, ffn_conv_b, w_down_bf16, ln_g, ln_b, tm, tf, alpha):
    B, S, D = x.shape
    dff = w_down_bf16.shape[0]
    const = lambda b, s: (0, 0)
    return pl.pallas_call(
        functools.partial(_ffn_kernel, tm=tm, tf=tf, dff=dff, alpha=alpha),
        grid=(B, S // tm),
        in_specs=[pl.BlockSpec((1, tm, D), lambda b, s: (b, s, 0)),
                  pl.BlockSpec(w_up_bf16.shape, const, pipeline_mode=pl.Buffered(1)),
                  pl.BlockSpec(ffn_conv_w.shape, const),
                  pl.BlockSpec(ffn_conv_b.shape, const),
                  pl.BlockSpec(w_down_bf16.shape, const, pipeline_mode=pl.Buffered(1)),
                  pl.BlockSpec(ln_g.shape, const),
                  pl.BlockSpec(ln_b.shape, const)],
        out_specs=pl.BlockSpec((1, tm, D), lambda b, s: (b, s, 0)),
        out_shape=jax.ShapeDtypeStruct((B, S, D), _F32),
        scratch_shapes=[pltpu.VMEM((tm, dff), _BF16),
                        pltpu.VMEM((CARRY_ROWS, 2 * dff), _F32)],
        compiler_params=pltpu.CompilerParams(
            dimension_semantics=("parallel", "arbitrary"), vmem_limit_bytes=VMEM_LIMIT_BYTES),
        name="conv_ffn",
    )(x, w_up_bf16, ffn_conv_w, ffn_conv_b, w_down_bf16, ln_g, ln_b)


def _block_diag(w_pool):
    g, c, d = w_pool.shape
    out = jnp.zeros((g * c, g * d), w_pool.dtype)
    for i in range(g):
        out = out.at[i * c:(i + 1) * c, i * d:(i + 1) * d].set(w_pool[i])
    return out


def kernel(x, w_in, conv_w, w_pool, pool_scale, w_out, ln1_g, ln1_b, w_up, ffn_conv_w, ffn_conv_b,
           w_down, ln2_g, ln2_b, rel_bias):
    depth = w_in.shape[0]
    B, S, D = x.shape
    L = MOBA_BLOCK
    assert S % L == 0 and w_in.shape[2] == 3 * ATTN_WIDTH + 3 * CONV_WIDTH + POOL_WIDTH
    alpha = (2.0 * depth) ** 0.25
    tm = 512
    tf = 256
    bias_t = _build_bias(rel_bias, S // L)
    for l in range(depth):
        qkv, cp, kmean = _in_proj(x, w_in[l].astype(_BF16), tm)
        attn = _attention(qkv, kmean.reshape(B, S // L, ATTN_WIDTH), bias_t)
        x = _mix(attn, cp, x, w_out[l].astype(_BF16), conv_w[l], _block_diag(w_pool[l]).astype(_BF16),
                 pool_scale[l][None, :], ln1_g[l][None, :], ln1_b[l][None, :], tm, alpha)
        x = _ffn(x, w_up[l].astype(_BF16), ffn_conv_w[l], ffn_conv_b[l][None, :], w_down[l].astype(_BF16),
                 ln2_g[l][None, :], ln2_b[l][None, :], tm, tf, alpha)
    return x
```

```python
import functools
import math

import jax
import jax.numpy as jnp
from jax import lax
from jax.experimental import pallas as pl
from jax.experimental.pallas import tpu as pltpu

HEAD_DIM = 64
N_ATTN_HEADS = 8
ATTN_WIDTH = N_ATTN_HEADS * HEAD_DIM
CONV_WIDTH = 256
POOL_WIDTH = 256
POOL_WINDOWS = (2, 4, 8, 16)
POOL_GROUP = POOL_WIDTH // len(POOL_WINDOWS)
CONV_K = 3
MOBA_BLOCK = 256
MOBA_TOP_K = 3
NUM_BUCKETS = 32
MAX_DISTANCE = 1024
LN_EPS = 1e-5
NEG_INF = -1e30
LOG2_E = 1.4426950408889634

LANES = 128
HEADS_PER_LANE_TILE = LANES // HEAD_DIM
HALO_ROWS = 16
CARRY_ROWS = 8
SCORE_CHUNK_ROWS = 32
GATE_ROWS = 16
VMEM_LIMIT_BYTES = 56 * 1024 * 1024

_BF16 = jnp.bfloat16
_F32 = jnp.float32


def _t5_bucket(dist):
    n = jnp.maximum(dist, 0)
    max_exact = NUM_BUCKETS // 2
    nf = jnp.maximum(n, 1).astype(jnp.float32)
    large = max_exact + (jnp.log(nf / max_exact) / math.log(MAX_DISTANCE / max_exact)
                         * (NUM_BUCKETS - max_exact)).astype(jnp.int32)
    large = jnp.minimum(large, NUM_BUCKETS - 1)
    return jnp.where(n < max_exact, n, large)


def _layer_norm(z, g, b):
    mu = jnp.mean(z, axis=-1, keepdims=True)
    zc = z - mu
    var = jnp.mean(zc * zc, axis=-1, keepdims=True)
    return zc * lax.rsqrt(var + LN_EPS) * g + b


def _bias_kernel(tab_ref, bucket_ref, out_ref):
    h = pl.program_id(0)
    d = pl.program_id(1)
    L = MOBA_BLOCK
    bucket = bucket_ref[0]
    acc = jnp.zeros((L, L), _F32)
    for b in range(NUM_BUCKETS):
        acc = jnp.where(bucket == b, tab_ref[h, b], acc)
    key = lax.broadcasted_iota(jnp.int32, (L, L), 0)
    qry = lax.broadcasted_iota(jnp.int32, (L, L), 1)
    out_ref[0, 0] = jnp.where(d * L + qry - key >= 0, acc * LOG2_E, NEG_INF)


def _build_bias(rel_bias, nb):
    L = MOBA_BLOCK
    key = jnp.arange(L, dtype=jnp.int32)[:, None]
    qry = jnp.arange(L, dtype=jnp.int32)[None, :]
    delta = jnp.arange(nb, dtype=jnp.int32)[:, None, None]
    bucket_t = _t5_bucket(delta * L + qry - key)
    return pl.pallas_call(
        _bias_kernel,
        grid=(N_ATTN_HEADS, nb),
        in_specs=[pl.BlockSpec(memory_space=pltpu.SMEM),
                  pl.BlockSpec((1, L, L), lambda h, d: (d, 0, 0))],
        out_specs=pl.BlockSpec((1, 1, L, L), lambda h, d: (h, d, 0, 0)),
        out_shape=jax.ShapeDtypeStruct((N_ATTN_HEADS, nb, L, L), _F32),
        name="rel_bias",
    )(rel_bias, bucket_t)


def _inproj_kernel(x_ref, w_ref, qkv_ref, cp_ref, kmean_ref, *, tm):
    L = MOBA_BLOCK
    xb = x_ref[0].astype(_BF16)
    scale = HEAD_DIM ** -0.5 * LOG2_E

    def proj(c0, width):
        return jnp.dot(xb, w_ref[:, c0:c0 + width], preferred_element_type=_F32)

    qkv_ref[0, :, 0:ATTN_WIDTH] = (proj(0, ATTN_WIDTH) * scale).astype(_BF16)
    k = proj(ATTN_WIDTH, ATTN_WIDTH)
    qkv_ref[0, :, ATTN_WIDTH:2 * ATTN_WIDTH] = k.astype(_BF16)
    for j in range(tm // L):
        kmean_ref[0, j] = jnp.mean(k[j * L:(j + 1) * L], axis=0, keepdims=True)
    qkv_ref[0, :, 2 * ATTN_WIDTH:3 * ATTN_WIDTH] = proj(2 * ATTN_WIDTH, ATTN_WIDTH).astype(_BF16)
    rest = 3 * CONV_WIDTH + POOL_WIDTH
    for c in range(0, rest, 512):
        cp_ref[0, :, c:c + 512] = proj(3 * ATTN_WIDTH + c, 512)


def _in_proj(x, w_in_bf16, tm):
    B, S, D = x.shape
    L = MOBA_BLOCK
    cols = w_in_bf16.shape[1]
    rest = cols - 3 * ATTN_WIDTH
    return pl.pallas_call(
        functools.partial(_inproj_kernel, tm=tm),
        grid=(B, S // tm),
        in_specs=[pl.BlockSpec((1, tm, D), lambda b, s: (b, s, 0)),
                  pl.BlockSpec((D, cols), lambda b, s: (0, 0))],
        out_specs=[pl.BlockSpec((1, tm, 3 * ATTN_WIDTH), lambda b, s: (b, s, 0)),
                   pl.BlockSpec((1, tm, rest), lambda b, s: (b, s, 0)),
                   pl.BlockSpec((1, tm // L, 1, ATTN_WIDTH), lambda b, s: (b, s, 0, 0))],
        out_shape=[jax.ShapeDtypeStruct((B, S, 3 * ATTN_WIDTH), _BF16),
                   jax.ShapeDtypeStruct((B, S, rest), _F32),
                   jax.ShapeDtypeStruct((B, S // L, 1, ATTN_WIDTH), _F32)],
        compiler_params=pltpu.CompilerParams(
            dimension_semantics=("parallel", "parallel"), vmem_limit_bytes=VMEM_LIMIT_BYTES),
        name="in_proj",
    )(x, w_in_bf16)


def _attn_kernel(q_ref, k_ref, v_ref, km_ref, bias_ref, o_ref, keys_ref, vals_ref, qt_ref,
                 z0_ref, z1_ref, p0_ref, p1_ref, *, nb):
    L = MOBA_BLOCK
    G = GATE_ROWS
    lane = lax.broadcasted_iota(jnp.int32, (1, LANES), 1)
    dim = lax.broadcasted_iota(jnp.int32, (LANES, 1), 0)
    blk = lax.broadcasted_iota(jnp.int32, (nb, 1), 0)
    NH = HEADS_PER_LANE_TILE
    CH = SCORE_CHUNK_ROWS
    z_refs, p_refs = (z0_ref, z1_ref), (p0_ref, p1_ref)
    ones_lo = (HEAD_DIM, HEAD_DIM - G)
    k_all = k_ref[0]
    km = km_ref[0]
    v_t = jnp.transpose(v_ref[0].astype(_F32))
    qt_ref[...] = jnp.transpose(q_ref[0].astype(_F32)).astype(_BF16)
    for hh in range(NH):
        in_head = (lane >= hh * HEAD_DIM) & (lane < (hh + 1) * HEAD_DIM)
        keys_ref[hh, 0:G, :] = jnp.concatenate(
            [jnp.where(in_head, km, 0.0), jnp.zeros((G - nb, LANES), _F32)], axis=0).astype(_BF16)
        keys_ref[hh, G:, :] = jnp.where(in_head, k_all, jnp.zeros((), _BF16))
        ones_rows = (dim >= ones_lo[hh]) & (dim < ones_lo[hh] + G)
        vals_ref[hh] = jnp.where(ones_rows, 1.0, v_t).astype(_BF16)

    def score_steps(c, i, hh, state):
        slot = c % 2
        n_sel = min(MOBA_TOP_K, i)

        def dot_step(j):
            lo = 0 if j == 0 else G + j * L
            s = jnp.dot(keys_ref[hh, lo:G + (j + 1) * L, :], qt_ref[:, i * L:(i + 1) * L],
                        preferred_element_type=_F32)
            if j == 0:
                if n_sel < i:
                    gate = s[:nb]
                    rank = jnp.zeros((nb, L), _F32)
                    for jp in range(i):
                        gj = gate[jp:jp + 1, :]
                        beats = (gj > gate) | ((gj == gate) & (blk > jp))
                        rank = rank + jnp.where(beats, 1.0, 0.0)
                    state["sel"] = jnp.where(rank < n_sel, 1.0, 0.0)
                s = s[G:]
            state["s"] = s
            if j < i and n_sel < i:
                state["keep"] = state["sel"][j:j + 1, :] > 0.5

        def chunk_step(j, r):
            zc = state["s"][r:r + CH] + bias_ref[hh, i - j, r:r + CH, :]
            if j < i and n_sel < i:
                zc = jnp.where(state["keep"], zc, NEG_INF)
            z_refs[slot][j * L + r:j * L + r + CH, :] = zc
            m8 = state.get("m8")
            for g in range(0, CH, 8):
                m8 = zc[g:g + 8] if m8 is None else jnp.maximum(m8, zc[g:g + 8])
            state["m8"] = m8
            if j == i and r == L - CH:
                state["m"] = jnp.max(m8, axis=0, keepdims=True)

        steps = []
        for j in range(i + 1):
            steps.append(functools.partial(dot_step, j))
            steps.extend(functools.partial(chunk_step, j, r) for r in range(0, L, CH))
        return steps

    def prob_steps(c, i, hh, state, outs):
        slot = c % 2
        n = (i + 1) * L

        def chunk_step(r):
            p_refs[slot][r:r + CH, :] = jnp.exp2(z_refs[slot][r:r + CH, :] - state["m"]).astype(_BF16)

        def pv_step():
            lo = min(hh * HEAD_DIM, ones_lo[hh])
            o_ext = jnp.dot(vals_ref[hh, lo:lo + HEAD_DIM + G, 0:n], p_refs[slot][0:n, :],
                            preferred_element_type=_F32)
            norm = ones_lo[hh] - lo
            head = hh * HEAD_DIM - lo
            outs[hh] = o_ext[head:head + HEAD_DIM] * (1.0 / o_ext[norm:norm + 1, :])
            if hh == NH - 1:
                o_t = jnp.concatenate([outs[0], outs[1]], axis=0)
                o_ref[0, i * L:(i + 1) * L, :] = jnp.transpose(o_t).astype(o_ref.dtype)

        return [functools.partial(chunk_step, r) for r in range(0, n, CH)] + [pv_step]

    chains = [(i, hh) for i in range(nb) for hh in range(NH)]
    pending = []
    outs = {}
    for c, (i, hh) in enumerate(chains):
        state = {}
        if hh == 0:
            outs = {}
        score = score_steps(c, i, hh, state)
        for t in range(max(len(score), len(pending))):
            if t < len(score):
                score[t]()
            if t < len(pending):
                pending[t]()
        pending = prob_steps(c, i, hh, state, outs)
    for step in pending:
        step()


def _attention(qkv, kmean, bias_t):
    B, S, _ = qkv.shape
    L = MOBA_BLOCK
    nb = S // L
    assert nb <= GATE_ROWS and HEADS_PER_LANE_TILE == 2
    n_tiles = ATTN_WIDTH // LANES
    return pl.pallas_call(
        functools.partial(_attn_kernel, nb=nb),
        grid=(n_tiles, B),
        in_specs=[pl.BlockSpec((1, S, LANES), lambda t, b: (b, 0, t)),
                  pl.BlockSpec((1, S, LANES), lambda t, b: (b, 0, n_tiles + t)),
                  pl.BlockSpec((1, S, LANES), lambda t, b: (b, 0, 2 * n_tiles + t)),
                  pl.BlockSpec((1, nb, LANES), lambda t, b: (b, 0, t)),
                  pl.BlockSpec((HEADS_PER_LANE_TILE, nb, L, L), lambda t, b: (t, 0, 0, 0))],
        out_specs=pl.BlockSpec((1, S, LANES), lambda t, b: (b, 0, t)),
        out_shape=jax.ShapeDtypeStruct((B, S, ATTN_WIDTH), _BF16),
        scratch_shapes=[pltpu.VMEM((HEADS_PER_LANE_TILE, GATE_ROWS + S, LANES), _BF16),
                        pltpu.VMEM((HEADS_PER_LANE_TILE, LANES, S), _BF16),
                        pltpu.VMEM((LANES, S), _BF16),
                        pltpu.VMEM((S, L), _F32), pltpu.VMEM((S, L), _F32),
                        pltpu.VMEM((S, L), _BF16), pltpu.VMEM((S, L), _BF16)],
        compiler_params=pltpu.CompilerParams(
            dimension_semantics=("parallel", "parallel"), vmem_limit_bytes=VMEM_LIMIT_BYTES),
        name="moba_attention",
    )(qkv, qkv, qkv, kmean, bias_t)


def _mixers(cp, halo, first_row, convw_ref, wpool_ref, pscale_ref):
    H = HALO_ROWS
    tm = cp.shape[0]
    c1, c2, c3 = CONV_WIDTH, 2 * CONV_WIDTH, 3 * CONV_WIDTH

    prod = cp[:, c1:c2] * cp[:, c2:c3]
    ext = jnp.concatenate([halo[:, c1:c2] * halo[:, c2:c3], prod], axis=0)
    cw = convw_ref[...]
    conv = cp[:, 0:c1] * (cw[0:1] * pltpu.roll(ext, 2, 0)[H:] + cw[1:2] * pltpu.roll(ext, 1, 0)[H:]
                          + cw[2:3] * prod)

    pin = cp[:, c3:]
    acc = jnp.concatenate([halo[:, c3:], pin], axis=0)
    col = lax.broadcasted_iota(jnp.int32, (1, POOL_WIDTH), 1)
    wsum = None
    wlen = None
    shift = 1
    for g, w in enumerate(POOL_WINDOWS):
        while shift < w:
            acc = acc + pltpu.roll(acc, shift, 0)
            shift *= 2
        if wsum is None:
            wsum, wlen = acc, jnp.full((1, POOL_WIDTH), float(w), _F32)
        else:
            in_group = col >= g * POOL_GROUP
            wsum = jnp.where(in_group, acc, wsum)
            wlen = jnp.where(in_group, float(w), wlen)
    pos = (first_row + 1 + lax.broadcasted_iota(jnp.int32, (tm, 1), 0)).astype(_F32)
    pooled = wsum[H:] / jnp.minimum(pos, wlen) - pin
    pool = jnp.dot(pooled.astype(_BF16), wpool_ref[...], preferred_element_type=_F32) * pscale_ref[...]
    return conv.astype(_BF16), pool.astype(_BF16)


def _tail_kernel(attn_ref, cp_ref, halo_ref, x_ref, wout_ref, convw_ref, wpool_ref, pscale_ref, g1_ref, b1_ref,
                 wup_ref, cw_ref, cb_ref, wdn_ref, g2_ref, b2_ref, o_ref, act_ref, carry_ref,
                 *, tm, tf, dff, alpha):
    s = pl.program_id(1)
    R = CARRY_ROWS
    halo = jnp.where(s > 0, halo_ref[0], 0.0)
    conv, pool = _mixers(cp_ref[0], halo, s * tm, convw_ref, wpool_ref, pscale_ref)
    mix_in = jnp.concatenate([attn_ref[0], conv, pool], axis=1)
    mix = jnp.dot(mix_in, wout_ref[...], preferred_element_type=_F32)
    x = _layer_norm(alpha * x_ref[0] + mix, g1_ref[...], b1_ref[...])
    xb = x.astype(_BF16)

    def conv_up(c0):
        up = jnp.dot(xb, wup_ref[:, c0:c0 + tf], preferred_element_type=_F32)
        prev = jnp.where(s > 0, carry_ref[:, c0:c0 + tf], 0.0)
        carry_ref[:, c0:c0 + tf] = up[tm - R:]
        ext = jnp.concatenate([prev, up], axis=0)
        w = cw_ref[:, c0:c0 + tf]
        return (w[0:1] * pltpu.roll(ext, 2, 0)[R:] + w[1:2] * pltpu.roll(ext, 1, 0)[R:]
                + w[2:3] * up + cb_ref[:, c0:c0 + tf])

    for f in range(dff // tf):
        u = conv_up(f * tf)
        g = conv_up(dff + f * tf)
        act_ref[:, f * tf:(f + 1) * tf] = (u * (g * jax.nn.sigmoid(g))).astype(_BF16)
    ff = jnp.dot(act_ref[...], wdn_ref[...], preferred_element_type=_F32)
    o_ref[0] = _layer_norm(alpha * x + ff, g2_ref[...], b2_ref[...])


def _tail(attn, cp, x, w_out_bf16, conv_w, w_pool_bd, pool_scale, ln1_g, ln1_b,
          w_up_bf16, ffn_conv_w, ffn_conv_b, w_down_bf16, ln2_g, ln2_b, tm, tf, alpha):
    B, S, D = x.shape
    H = HALO_ROWS
    cpw = cp.shape[2]
    dff = w_down_bf16.shape[0]
    params = (w_out_bf16, conv_w, w_pool_bd, pool_scale, ln1_g, ln1_b,
              w_up_bf16, ffn_conv_w, ffn_conv_b, w_down_bf16, ln2_g, ln2_b)
    return pl.pallas_call(
        functools.partial(_tail_kernel, tm=tm, tf=tf, dff=dff, alpha=alpha),
        grid=(B, S // tm),
        in_specs=[pl.BlockSpec((1, tm, ATTN_WIDTH), lambda b, s: (b, s, 0)),
                  pl.BlockSpec((1, tm, cpw), lambda b, s: (b, s, 0)),
                  pl.BlockSpec((1, H, cpw), lambda b, s: (b, jnp.maximum(s * (tm // H) - 1, 0), 0)),
                  pl.BlockSpec((1, tm, D), lambda b, s: (b, s, 0))]
                 + [pl.BlockSpec(a.shape, lambda b, s: (0, 0), pipeline_mode=pl.Buffered(1)) for a in params],
        out_specs=pl.BlockSpec((1, tm, D), lambda b, s: (b, s, 0)),
        out_shape=jax.ShapeDtypeStruct((B, S, D), _F32),
        scratch_shapes=[pltpu.VMEM((tm, dff), _BF16),
                        pltpu.VMEM((CARRY_ROWS, 2 * dff), _F32)],
        compiler_params=pltpu.CompilerParams(
            dimension_semantics=("parallel", "arbitrary"), vmem_limit_bytes=VMEM_LIMIT_BYTES),
        name="layer_tail",
    )(attn, cp, cp, x, *params)


def _block_diag(w_pool):
    g, c, d = w_pool.shape
    out = jnp.zeros((g * c, g * d), w_pool.dtype)
    for i in range(g):
        out = out.at[i * c:(i + 1) * c, i * d:(i + 1) * d].set(w_pool[i])
    return out


def kernel(x, w_in, conv_w, w_pool, pool_scale, w_out, ln1_g, ln1_b, w_up, ffn_conv_w, ffn_conv_b,
           w_down, ln2_g, ln2_b, rel_bias):
    depth = w_in.shape[0]
    B, S, D = x.shape
    L = MOBA_BLOCK
    assert S % L == 0 and w_in.shape[2] == 3 * ATTN_WIDTH + 3 * CONV_WIDTH + POOL_WIDTH
    alpha = (2.0 * depth) ** 0.25
    tm = 512
    tf = 256
    bias_t = _build_bias(rel_bias, S // L)
    for l in range(depth):
        qkv, cp, kmean = _in_proj(x, w_in[l].astype(_BF16), tm)
        attn = _attention(qkv, kmean.reshape(B, S // L, ATTN_WIDTH), bias_t)
        x = _tail(attn, cp, x, w_out[l].astype(_BF16), conv_w[l], _block_diag(w_pool[l]).astype(_BF16),
                  pool_scale[l][None, :], ln1_g[l][None, :], ln1_b[l][None, :],
                  w_up[l].astype(_BF16), ffn_conv_w[l], ffn_conv_b[l][None, :], w_down[l].astype(_BF16),
                  ln2_g[l][None, :], ln2_b[l][None, :], tm, tf, alpha)
    return x
```

```python
import functools
import math

import jax
import jax.numpy as jnp
from jax import lax
from jax.experimental import pallas as pl
from jax.experimental.pallas import tpu as pltpu

HEAD_DIM = 64
N_ATTN_HEADS = 8
ATTN_WIDTH = N_ATTN_HEADS * HEAD_DIM
CONV_WIDTH = 256
POOL_WIDTH = 256
POOL_WINDOWS = (2, 4, 8, 16)
POOL_GROUP = POOL_WIDTH // len(POOL_WINDOWS)
CONV_K = 3
MOBA_BLOCK = 256
MOBA_TOP_K = 3
NUM_BUCKETS = 32
MAX_DISTANCE = 1024
LN_EPS = 1e-5
NEG_INF = -1e30
LOG2_E = 1.4426950408889634

LANES = 128
HEADS_PER_LANE_TILE = LANES // HEAD_DIM
HALO_ROWS = 16
CARRY_ROWS = 8
BIAS_CHUNK_ROWS = 16
SCORE_CHUNK_ROWS = 32
GATE_ROWS = 16
VMEM_LIMIT_BYTES = 56 * 1024 * 1024

_BF16 = jnp.bfloat16
_F32 = jnp.float32


def _t5_bucket(dist):
    n = jnp.maximum(dist, 0)
    max_exact = NUM_BUCKETS // 2
    nf = jnp.maximum(n, 1).astype(jnp.float32)
    large = max_exact + (jnp.log(nf / max_exact) / math.log(MAX_DISTANCE / max_exact)
                         * (NUM_BUCKETS - max_exact)).astype(jnp.int32)
    large = jnp.minimum(large, NUM_BUCKETS - 1)
    return jnp.where(n < max_exact, n, large)


def _layer_norm(z, g, b):
    mu = jnp.mean(z, axis=-1, keepdims=True)
    zc = z - mu
    var = jnp.mean(zc * zc, axis=-1, keepdims=True)
    return zc * lax.rsqrt(var + LN_EPS) * g + b


def _bucket_span(dist_lo, dist_hi):
    def bucket(n):
        n = max(n, 0)
        half = NUM_BUCKETS // 2
        if n < half:
            return n
        return min(half + int(math.log(n / half) / math.log(MAX_DISTANCE / half) * (NUM_BUCKETS - half)),
                   NUM_BUCKETS - 1)
    return max(bucket(dist_lo) - 1, 0), min(bucket(dist_hi) + 1, NUM_BUCKETS - 1)


def _bias_kernel(tab_ref, bucket_ref, out_ref, *, nb):
    L = MOBA_BLOCK
    rows = BIAS_CHUNK_ROWS

    def build(d):
        for r in range(0, L, rows):
            bucket = bucket_ref[0, r:r + rows, :]
            first, last = _bucket_span(d * L - (r + rows - 1), d * L + (L - 1) - r)
            accs = [jnp.full((rows, L), tab_ref[h, first], _F32) for h in range(N_ATTN_HEADS)]
            for b in range(first + 1, last + 1):
                hit = bucket == b
                accs = [jnp.where(hit, tab_ref[h, b], acc) for h, acc in enumerate(accs)]
            key = r + lax.broadcasted_iota(jnp.int32, (rows, L), 0)
            qry = lax.broadcasted_iota(jnp.int32, (rows, L), 1)
            causal = d * L + qry - key >= 0
            for h, acc in enumerate(accs):
                out_ref[h, 0, r:r + rows, :] = jnp.where(causal, acc * LOG2_E, NEG_INF)

    for d in range(nb):
        pl.when(pl.program_id(0) == d)(functools.partial(build, d))


def _build_bias(rel_bias, nb):
    L = MOBA_BLOCK
    key = jnp.arange(L, dtype=jnp.int32)[:, None]
    qry = jnp.arange(L, dtype=jnp.int32)[None, :]
    delta = jnp.arange(nb, dtype=jnp.int32)[:, None, None]
    bucket_t = _t5_bucket(delta * L + qry - key)
    return pl.pallas_call(
        functools.partial(_bias_kernel, nb=nb),
        grid=(nb,),
        in_specs=[pl.BlockSpec(memory_space=pltpu.SMEM),
                  pl.BlockSpec((1, L, L), lambda d: (d, 0, 0))],
        out_specs=pl.BlockSpec((N_ATTN_HEADS, 1, L, L), lambda d: (0, d, 0, 0)),
        out_shape=jax.ShapeDtypeStruct((N_ATTN_HEADS, nb, L, L), _F32),
        name="rel_bias",
    )(rel_bias, bucket_t)


def _inproj_kernel(x_ref, w_ref, qkv_ref, cp_ref, kmean_ref, *, tm):
    L = MOBA_BLOCK
    xb = x_ref[0].astype(_BF16)
    scale = HEAD_DIM ** -0.5 * LOG2_E

    def proj(c0, width):
        return jnp.dot(xb, w_ref[:, c0:c0 + width], preferred_element_type=_F32)

    qkv_ref[0, :, 0:ATTN_WIDTH] = (proj(0, ATTN_WIDTH) * scale).astype(_BF16)
    k = proj(ATTN_WIDTH, ATTN_WIDTH)
    qkv_ref[0, :, ATTN_WIDTH:2 * ATTN_WIDTH] = k.astype(_BF16)
    for j in range(tm // L):
        kmean_ref[0, j] = jnp.mean(k[j * L:(j + 1) * L], axis=0, keepdims=True)
    qkv_ref[0, :, 2 * ATTN_WIDTH:3 * ATTN_WIDTH] = proj(2 * ATTN_WIDTH, ATTN_WIDTH).astype(_BF16)
    rest = 3 * CONV_WIDTH + POOL_WIDTH
    for c in range(0, rest, 512):
        cp_ref[0, :, c:c + 512] = proj(3 * ATTN_WIDTH + c, 512)


def _in_proj(x, w_in_bf16, layer, tm):
    B, S, D = x.shape
    L = MOBA_BLOCK
    cols = w_in_bf16.shape[2]
    rest = cols - 3 * ATTN_WIDTH
    return pl.pallas_call(
        functools.partial(_inproj_kernel, tm=tm),
        grid=(B, S // tm),
        in_specs=[pl.BlockSpec((1, tm, D), lambda b, s: (b, s, 0)),
                  pl.BlockSpec((None, D, cols), lambda b, s: (layer, 0, 0))],
        out_specs=[pl.BlockSpec((1, tm, 3 * ATTN_WIDTH), lambda b, s: (b, s, 0)),
                   pl.BlockSpec((1, tm, rest), lambda b, s: (b, s, 0)),
                   pl.BlockSpec((1, tm // L, 1, ATTN_WIDTH), lambda b, s: (b, s, 0, 0))],
        out_shape=[jax.ShapeDtypeStruct((B, S, 3 * ATTN_WIDTH), _BF16),
                   jax.ShapeDtypeStruct((B, S, rest), _F32),
                   jax.ShapeDtypeStruct((B, S // L, 1, ATTN_WIDTH), _F32)],
        compiler_params=pltpu.CompilerParams(
            dimension_semantics=("parallel", "parallel"), vmem_limit_bytes=VMEM_LIMIT_BYTES),
        name="in_proj",
    )(x, w_in_bf16)


def _attn_kernel(q_ref, k_ref, v_ref, km_ref, bias_ref, o_ref, keys_ref, vals_ref, qt_ref,
                 z0_ref, z1_ref, p0_ref, p1_ref, *, nb):
    L = MOBA_BLOCK
    G = GATE_ROWS
    lane = lax.broadcasted_iota(jnp.int32, (1, LANES), 1)
    dim = lax.broadcasted_iota(jnp.int32, (LANES, 1), 0)
    blk = lax.broadcasted_iota(jnp.int32, (nb, 1), 0)
    NH = HEADS_PER_LANE_TILE
    CH = SCORE_CHUNK_ROWS
    z_refs, p_refs = (z0_ref, z1_ref), (p0_ref, p1_ref)
    ones_lo = (HEAD_DIM, HEAD_DIM - G)
    k_all = k_ref[0]
    km = km_ref[0]
    v_t = jnp.transpose(v_ref[0].astype(_F32))
    qt_ref[...] = jnp.transpose(q_ref[0].astype(_F32)).astype(_BF16)
    for hh in range(NH):
        in_head = (lane >= hh * HEAD_DIM) & (lane < (hh + 1) * HEAD_DIM)
        keys_ref[hh, 0:G, :] = jnp.concatenate(
            [jnp.where(in_head, km, 0.0), jnp.zeros((G - nb, LANES), _F32)], axis=0).astype(_BF16)
        keys_ref[hh, G:, :] = jnp.where(in_head, k_all, jnp.zeros((), _BF16))
        ones_rows = (dim >= ones_lo[hh]) & (dim < ones_lo[hh] + G)
        vals_ref[hh] = jnp.where(ones_rows, 1.0, v_t).astype(_BF16)

    def score_steps(c, i, hh, state):
        slot = c % 2
        n_sel = min(MOBA_TOP_K, i)

        def dot_step(j):
            lo = 0 if j == 0 else G + j * L
            s = jnp.dot(keys_ref[hh, lo:G + (j + 1) * L, :], qt_ref[:, i * L:(i + 1) * L],
                        preferred_element_type=_F32)
            if j == 0:
                if n_sel < i:
                    gate = s[:nb]
                    rank = jnp.zeros((nb, L), _F32)
                    for jp in range(i):
                        gj = gate[jp:jp + 1, :]
                        beats = (gj > gate) | ((gj == gate) & (blk > jp))
                        rank = rank + jnp.where(beats, 1.0, 0.0)
                    state["sel"] = jnp.where(rank < n_sel, 1.0, 0.0)
                s = s[G:]
            state["s"] = s
            if j < i and n_sel < i:
                state["keep"] = state["sel"][j:j + 1, :] > 0.5

        def chunk_step(j, r):
            zc = state["s"][r:r + CH] + bias_ref[hh, i - j, r:r + CH, :]
            if j < i and n_sel < i:
                zc = jnp.where(state["keep"], zc, NEG_INF)
            z_refs[slot][j * L + r:j * L + r + CH, :] = zc
            m8 = state.get("m8")
            for g in range(0, CH, 8):
                m8 = zc[g:g + 8] if m8 is None else jnp.maximum(m8, zc[g:g + 8])
            state["m8"] = m8
            if j == i and r == L - CH:
                state["m"] = jnp.max(m8, axis=0, keepdims=True)

        steps = []
        for j in range(i + 1):
            steps.append(functools.partial(dot_step, j))
            steps.extend(functools.partial(chunk_step, j, r) for r in range(0, L, CH))
        return steps

    def prob_steps(c, i, hh, state, outs):
        slot = c % 2
        n = (i + 1) * L

        def chunk_step(r):
            p_refs[slot][r:r + CH, :] = jnp.exp2(z_refs[slot][r:r + CH, :] - state["m"]).astype(_BF16)

        def pv_step():
            lo = min(hh * HEAD_DIM, ones_lo[hh])
            o_ext = jnp.dot(vals_ref[hh, lo:lo + HEAD_DIM + G, 0:n], p_refs[slot][0:n, :],
                            preferred_element_type=_F32)
            norm = ones_lo[hh] - lo
            head = hh * HEAD_DIM - lo
            outs[hh] = o_ext[head:head + HEAD_DIM] * (1.0 / o_ext[norm:norm + 1, :])
            if hh == NH - 1:
                o_t = jnp.concatenate([outs[0], outs[1]], axis=0)
                o_ref[0, i * L:(i + 1) * L, :] = jnp.transpose(o_t).astype(o_ref.dtype)

        return [functools.partial(chunk_step, r) for r in range(0, n, CH)] + [pv_step]

    chains = [(i, hh) for i in range(nb) for hh in range(NH)]
    pending = []
    outs = {}
    for c, (i, hh) in enumerate(chains):
        state = {}
        if hh == 0:
            outs = {}
        score = score_steps(c, i, hh, state)
        for t in range(max(len(score), len(pending))):
            if t < len(score):
                score[t]()
            if t < len(pending):
                pending[t]()
        pending = prob_steps(c, i, hh, state, outs)
    for step in pending:
        step()


def _attention(qkv, kmean, bias_t):
    B, S, _ = qkv.shape
    L = MOBA_BLOCK
    nb = S // L
    assert nb <= GATE_ROWS and HEADS_PER_LANE_TILE == 2
    n_tiles = ATTN_WIDTH // LANES
    return pl.pallas_call(
        functools.partial(_attn_kernel, nb=nb),
        grid=(n_tiles, B),
        in_specs=[pl.BlockSpec((1, S, LANES), lambda t, b: (b, 0, t)),
                  pl.BlockSpec((1, S, LANES), lambda t, b: (b, 0, n_tiles + t)),
                  pl.BlockSpec((1, S, LANES), lambda t, b: (b, 0, 2 * n_tiles + t)),
                  pl.BlockSpec((1, nb, LANES), lambda t, b: (b, 0, t)),
                  pl.BlockSpec((HEADS_PER_LANE_TILE, nb, L, L), lambda t, b: (t, 0, 0, 0))],
        out_specs=pl.BlockSpec((1, S, LANES), lambda t, b: (b, 0, t)),
        out_shape=jax.ShapeDtypeStruct((B, S, ATTN_WIDTH), _BF16),
        scratch_shapes=[pltpu.VMEM((HEADS_PER_LANE_TILE, GATE_ROWS + S, LANES), _BF16),
                        pltpu.VMEM((HEADS_PER_LANE_TILE, LANES, S), _BF16),
                        pltpu.VMEM((LANES, S), _BF16),
                        pltpu.VMEM((S, L), _F32), pltpu.VMEM((S, L), _F32),
                        pltpu.VMEM((S, L), _BF16), pltpu.VMEM((S, L), _BF16)],
        compiler_params=pltpu.CompilerParams(
            dimension_semantics=("parallel", "parallel"), vmem_limit_bytes=VMEM_LIMIT_BYTES),
        name="moba_attention",
    )(qkv, qkv, qkv, kmean, bias_t)


def _mixers(cp, halo, first_row, convw_ref, wpool_ref, pscale_ref):
    H = HALO_ROWS
    tm = cp.shape[0]
    c1, c2, c3 = CONV_WIDTH, 2 * CONV_WIDTH, 3 * CONV_WIDTH

    prod = cp[:, c1:c2] * cp[:, c2:c3]
    ext = jnp.concatenate([halo[:, c1:c2] * halo[:, c2:c3], prod], axis=0)
    cw = convw_ref[...]
    conv = cp[:, 0:c1] * (cw[0:1] * pltpu.roll(ext, 2, 0)[H:] + cw[1:2] * pltpu.roll(ext, 1, 0)[H:]
                          + cw[2:3] * prod)

    pin = cp[:, c3:]
    acc = jnp.concatenate([halo[:, c3:], pin], axis=0)
    col = lax.broadcasted_iota(jnp.int32, (1, POOL_WIDTH), 1)
    wsum = None
    wlen = None
    shift = 1
    for g, w in enumerate(POOL_WINDOWS):
        while shift < w:
            acc = acc + pltpu.roll(acc, shift, 0)
            shift *= 2
        if wsum is None:
            wsum, wlen = acc, jnp.full((1, POOL_WIDTH), float(w), _F32)
        else:
            in_group = col >= g * POOL_GROUP
            wsum = jnp.where(in_group, acc, wsum)
            wlen = jnp.where(in_group, float(w), wlen)
    pos = (first_row + 1 + lax.broadcasted_iota(jnp.int32, (tm, 1), 0)).astype(_F32)
    pooled = wsum[H:] / jnp.minimum(pos, wlen) - pin
    pool = jnp.dot(pooled.astype(_BF16), wpool_ref[...], preferred_element_type=_F32) * pscale_ref[...]
    return conv.astype(_BF16), pool.astype(_BF16)


def _tail_kernel(attn_ref, cp_ref, halo_ref, x_ref, wout_ref, convw_ref, wpool_ref, pscale_ref, g1_ref, b1_ref,
                 wup_ref, cw_ref, cb_ref, wdn_ref, g2_ref, b2_ref, o_ref, act_ref, carry_ref,
                 *, tm, tf, dff, alpha):
    s = pl.program_id(1)
    R = CARRY_ROWS
    halo = jnp.where(s > 0, halo_ref[0], 0.0)
    conv, pool = _mixers(cp_ref[0], halo, s * tm, convw_ref, wpool_ref, pscale_ref)
    mix_in = jnp.concatenate([attn_ref[0], conv, pool], axis=1)
    mix = jnp.dot(mix_in, wout_ref[...], preferred_element_type=_F32)
    x = _layer_norm(alpha * x_ref[0] + mix, g1_ref[...], b1_ref[...])
    xb = x.astype(_BF16)

    def conv_up(c0):
        up = jnp.dot(xb, wup_ref[:, c0:c0 + tf], preferred_element_type=_F32)
        prev = jnp.where(s > 0, carry_ref[:, c0:c0 + tf], 0.0)
        carry_ref[:, c0:c0 + tf] = up[tm - R:]
        ext = jnp.concatenate([prev, up], axis=0)
        w = cw_ref[:, c0:c0 + tf]
        return (w[0:1] * pltpu.roll(ext, 2, 0)[R:] + w[1:2] * pltpu.roll(ext, 1, 0)[R:]
                + w[2:3] * up + cb_ref[:, c0:c0 + tf])

    for f in range(dff // tf):
        u = conv_up(f * tf)
        g = conv_up(dff + f * tf)
        act_ref[:, f * tf:(f + 1) * tf] = (u * (g * jax.nn.sigmoid(g))).astype(_BF16)
    ff = jnp.dot(act_ref[...], wdn_ref[...], preferred_element_type=_F32)
    o_ref[0] = _layer_norm(alpha * x + ff, g2_ref[...], b2_ref[...])


def _tail(attn, cp, x, params, layer, tm, tf, alpha):
    B, S, D = x.shape
    H = HALO_ROWS
    cpw = cp.shape[2]
    dff = params[9].shape[1]
    return pl.pallas_call(
        functools.partial(_tail_kernel, tm=tm, tf=tf, dff=dff, alpha=alpha),
        grid=(B, S // tm),
        in_specs=[pl.BlockSpec((1, tm, ATTN_WIDTH), lambda b, s: (b, s, 0)),
                  pl.BlockSpec((1, tm, cpw), lambda b, s: (b, s, 0)),
                  pl.BlockSpec((1, H, cpw), lambda b, s: (b, jnp.maximum(s * (tm // H) - 1, 0), 0)),
                  pl.BlockSpec((1, tm, D), lambda b, s: (b, s, 0))]
                 + [pl.BlockSpec((None,) + a.shape[1:], lambda b, s: (layer, 0, 0), pipeline_mode=pl.Buffered(1))
                    for a in params],
        out_specs=pl.BlockSpec((1, tm, D), lambda b, s: (b, s, 0)),
        out_shape=jax.ShapeDtypeStruct((B, S, D), _F32),
        scratch_shapes=[pltpu.VMEM((tm, dff), _BF16),
                        pltpu.VMEM((CARRY_ROWS, 2 * dff), _F32)],
        compiler_params=pltpu.CompilerParams(
            dimension_semantics=("parallel", "arbitrary"), vmem_limit_bytes=VMEM_LIMIT_BYTES),
        name="layer_tail",
    )(attn, cp, cp, x, *params)


def _block_diag(w_pool):
    depth, g, c, d = w_pool.shape
    eye = jnp.eye(g, dtype=w_pool.dtype)
    return (w_pool[:, :, :, None, :] * eye[None, :, None, :, None]).reshape(depth, g * c, g * d)


def kernel(x, w_in, conv_w, w_pool, pool_scale, w_out, ln1_g, ln1_b, w_up, ffn_conv_w, ffn_conv_b,
           w_down, ln2_g, ln2_b, rel_bias):
    depth = w_in.shape[0]
    B, S, D = x.shape
    L = MOBA_BLOCK
    assert S % L == 0 and w_in.shape[2] == 3 * ATTN_WIDTH + 3 * CONV_WIDTH + POOL_WIDTH
    alpha = (2.0 * depth) ** 0.25
    tm = 512
    tf = 256
    bias_t = _build_bias(rel_bias, S // L)
    row = lambda a: a[:, None, :]
    w_in_bf16 = w_in.astype(_BF16)
    tail_params = (w_out.astype(_BF16), conv_w, _block_diag(w_pool).astype(_BF16), row(pool_scale),
                   row(ln1_g), row(ln1_b), w_up.astype(_BF16), ffn_conv_w, row(ffn_conv_b),
                   w_down.astype(_BF16), row(ln2_g), row(ln2_b))
    for l in range(depth):
        qkv, cp, kmean = _in_proj(x, w_in_bf16, l, tm)
        attn = _attention(qkv, kmean.reshape(B, S // L, ATTN_WIDTH), bias_t)
        x = _tail(attn, cp, x, tail_params, l, tm, tf, alpha)
    return x
```
